```python
import jax, jax.numpy as jnp
from jax import lax
import numpy as np

D_MODEL = 4096
BATCH = 4
SEQ = 4096
DEPTH = 1

CHUNK = 64
N_BRANCH = 2
MIX_WIDTH = D_MODEL
BRANCH_WIDTH = MIX_WIDTH // 2
GMLP_WIDTH = BRANCH_WIDTH
GMLP_BLOCK = 128
GMLP_GROUP_DIM = 128
GMLP_GROUPS = GMLP_WIDTH // GMLP_GROUP_DIM
HG_DK = 128
HG_DV = 128
HG_WIDTH = BRANCH_WIDTH
HG_HEADS = HG_WIDTH // HG_DK
PROJ_COLS = 3 * GMLP_WIDTH + 5 * HG_WIDTH + N_BRANCH * D_MODEL
EPS = 1e-6

kernel_name = "gmlp_hgrn2_gated_hybrid_block"


def rmsnorm(x, w):
    x32 = x.astype(jnp.float32)
    y = x32 * lax.rsqrt(jnp.mean(x32 * x32, axis=-1, keepdims=True) + EPS)
    return y.astype(x.dtype) * w


def layernorm(x, w, b):
    x32 = x.astype(jnp.float32)
    mu = jnp.mean(x32, axis=-1, keepdims=True)
    xc = x32 - mu
    y = xc * lax.rsqrt(jnp.mean(xc * xc, axis=-1, keepdims=True) + EPS)
    return y.astype(x.dtype) * w + b


def gmlp_spatial_gate(u, v, ln_w, ln_b, w_s, b_s):
    bsz, seq, width = v.shape
    nb = seq // GMLP_BLOCK
    v = layernorm(v, ln_w, ln_b).reshape(bsz, nb, GMLP_BLOCK, GMLP_GROUPS, GMLP_GROUP_DIM)
    chunk_id = jnp.arange(GMLP_BLOCK) // CHUNK
    mask = chunk_id[None, :] <= chunk_id[:, None]
    w_m = jnp.where(mask[None], w_s, jnp.zeros((), w_s.dtype))
    mixed = jnp.einsum('gij,bnjgc->bnigc', w_m, v) + b_s.T[:, :, None]
    return u * mixed.reshape(bsz, seq, width)


def hgrn2_chunkwise(q, f_logit, inp, lb):
    bsz, seq, _ = q.shape
    n = seq // CHUNK
    f = lb + (1.0 - lb) * jax.nn.sigmoid(f_logit.astype(jnp.float32))
    g = jnp.log(f).reshape(bsz, n, CHUNK, HG_HEADS, HG_DK)
    k = (1.0 - f).reshape(bsz, n, CHUNK, HG_HEADS, HG_DK)
    qf = jax.nn.silu(q.astype(jnp.float32)).reshape(bsz, n, CHUNK, HG_HEADS, HG_DK)
    v = inp.astype(jnp.float32).reshape(bsz, n, CHUNK, HG_HEADS, HG_DV)
    c = jnp.cumsum(g, axis=2)
    c_mid = c[:, :, CHUNK // 2 - 1:CHUNK // 2]
    q_in = qf * jnp.exp(c - c_mid)
    k_in = k * jnp.exp(c_mid - c)
    scores = jnp.einsum('bnihd,bnjhd->bnhij', q_in, k_in)
    causal = jnp.tril(jnp.ones((CHUNK, CHUNK), dtype=bool))
    scores = jnp.where(causal, scores, 0.0)
    o_intra = jnp.einsum('bnhij,bnjhv->bnihv', scores, v)
    c_last = c[:, :, -1:]
    q_dec = qf * jnp.exp(c)
    k_dec = k * jnp.exp(c_last - c)
    chunk_decay = jnp.exp(c_last[:, :, 0])

    def step(state, xs):
        qd, kd, vv, dec = xs
        o = jnp.einsum('bihd,bhdv->bihv', qd, state)
        state = dec[..., None] * state + jnp.einsum('bjhd,bjhv->bhdv', kd, vv)
        return state, o

    s0 = jnp.zeros((bsz, HG_HEADS, HG_DK, HG_DV), jnp.float32)
    xs = (jnp.moveaxis(q_dec, 1, 0), jnp.moveaxis(k_dec, 1, 0),
          jnp.moveaxis(v, 1, 0), jnp.moveaxis(chunk_decay, 1, 0))
    _, o_inter = lax.scan(step, s0, xs)
    o = o_intra + jnp.moveaxis(o_inter, 0, 1)
    return o.reshape(bsz, seq, HG_HEADS, HG_DV)


def setup_inputs(seed: int = 0) -> dict:
    key = jax.random.key(seed)
    ks = jax.random.split(key, 14)
    f32 = jnp.float32
    x = jax.random.normal(ks[0], (BATCH, SEQ, D_MODEL), f32)
    norm_w = 1.0 + 0.02 * jax.random.normal(ks[1], (DEPTH, D_MODEL), f32)
    w_in = jax.random.normal(ks[2], (DEPTH, D_MODEL, PROJ_COLS), f32) * D_MODEL ** -0.5
    gmlp_ln_w = 1.0 + 0.02 * jax.random.normal(ks[3], (DEPTH, GMLP_WIDTH), f32)
    gmlp_ln_b = 0.02 * jax.random.normal(ks[4], (DEPTH, GMLP_WIDTH), f32)
    gmlp_w_s = jax.random.normal(ks[5], (DEPTH, GMLP_GROUPS, GMLP_BLOCK, GMLP_BLOCK), f32) * GMLP_BLOCK ** -0.5
    gmlp_b_s = 1.0 + 0.02 * jax.random.normal(ks[6], (DEPTH, GMLP_GROUPS, GMLP_BLOCK), f32)
    hgrn_lb_logits = 0.1 * jax.random.normal(ks[7], (DEPTH + 1, HG_WIDTH), f32)
    hgrn_norm_w = 1.0 + 0.02 * jax.random.normal(ks[8], (DEPTH, HG_WIDTH), f32)
    w_branch = jax.random.normal(ks[9], (DEPTH, N_BRANCH, BRANCH_WIDTH, D_MODEL), f32) * BRANCH_WIDTH ** -0.5
    w_out = jax.random.normal(ks[10], (DEPTH, D_MODEL, D_MODEL), f32) * D_MODEL ** -0.5
    final_norm_w = 1.0 + 0.02 * jax.random.normal(ks[11], (D_MODEL,), f32)
    return {"x": x, "norm_w": norm_w, "w_in": w_in, "gmlp_ln_w": gmlp_ln_w, "gmlp_ln_b": gmlp_ln_b,
            "gmlp_w_s": gmlp_w_s, "gmlp_b_s": gmlp_b_s, "hgrn_lb_logits": hgrn_lb_logits,
            "hgrn_norm_w": hgrn_norm_w, "w_branch": w_branch, "w_out": w_out,
            "final_norm_w": final_norm_w}


def reference(x, norm_w, w_in, gmlp_ln_w, gmlp_ln_b, gmlp_w_s, gmlp_b_s, hgrn_lb_logits,
              hgrn_norm_w, w_branch, w_out, final_norm_w):
    bsz, seq, _ = x.shape
    lb_all = jnp.cumsum(jax.nn.softmax(hgrn_lb_logits.astype(jnp.float32), axis=0), axis=0)
    split_pts = np.cumsum([GMLP_WIDTH] * 3 + [HG_WIDTH] * 5)
    for l in range(DEPTH):
        h = rmsnorm(x, norm_w[l])
        proj = h @ w_in[l]
        u, v, z_a, q, f_logit, inp, og, z_b, gates = jnp.split(proj, split_pts, axis=-1)
        h_a = gmlp_spatial_gate(jax.nn.gelu(u), jax.nn.gelu(v), gmlp_ln_w[l], gmlp_ln_b[l],
                                gmlp_w_s[l], gmlp_b_s[l]) * jax.nn.silu(z_a)
        o = hgrn2_chunkwise(q, f_logit, inp, lb_all[l])
        o32 = o * lax.rsqrt(jnp.mean(o * o, axis=-1, keepdims=True) + EPS)
        o = o32.reshape(bsz, seq, HG_WIDTH).astype(x.dtype) * hgrn_norm_w[l]
        h_b = o * jax.nn.sigmoid(og) * jax.nn.silu(z_b)
        g_a, g_b = jnp.split(jax.nn.sigmoid(gates), N_BRANCH, axis=-1)
        merged = g_a * (h_a @ w_branch[l, 0]) + g_b * (h_b @ w_branch[l, 1])
        x = x + merged @ w_out[l]
    return rmsnorm(x, final_norm_w)
```

```python
import functools

import jax
import jax.numpy as jnp
from jax import lax
from jax.experimental import pallas as pl
from jax.experimental.pallas import tpu as pltpu

F32 = jnp.float32
BF16 = jnp.bfloat16

EPS = 1e-6
CHUNK = 64
GMLP_BLOCK = 128
GROUP_DIM = 128
VMEM_LIMIT_BYTES = 60000 * 1024


def _params(*semantics):
    return pltpu.CompilerParams(dimension_semantics=semantics,
                                vmem_limit_bytes=VMEM_LIMIT_BYTES)


def _rmsnorm_kernel(x_ref, w_ref, o_ref):
    x = x_ref[...]
    ms = jnp.mean(x * x, axis=-1, keepdims=True)
    o_ref[...] = (x * lax.rsqrt(ms + EPS) * w_ref[...]).astype(o_ref.dtype)


def _rmsnorm(x2, w, tm=256):
    t, d = x2.shape
    return pl.pallas_call(
        _rmsnorm_kernel,
        grid=(t // tm,),
        in_specs=[pl.BlockSpec((tm, d), lambda i: (i, 0)),
                  pl.BlockSpec((1, d), lambda i: (0, 0))],
        out_specs=pl.BlockSpec((tm, d), lambda i: (i, 0)),
        out_shape=jax.ShapeDtypeStruct((t, d), BF16),
        compiler_params=_params("parallel"),
        name="prenorm",
    )(x2, w)


def _proj_kernel(*refs, n_w, n_extra, epilogue):
    a_ref = refs[0]
    w_refs = refs[1:1 + n_w]
    extra_refs = refs[1 + n_w:1 + n_w + n_extra]
    out_refs = refs[1 + n_w + n_extra:]
    a = a_ref[...]
    accs = [jnp.dot(a, w[...], preferred_element_type=F32) for w in w_refs]
    outs = epilogue(*accs, *[e[...] for e in extra_refs])
    for o_ref, o in zip(out_refs, outs):
        o_ref[...] = o.astype(o_ref.dtype)


def _proj(h, w, col_offsets, n_cols, epilogue, out_dtypes, extras=(), tm=1024, tn=1024, name="proj"):
    t, k = h.shape
    in_specs = [pl.BlockSpec((tm, k), lambda i, j: (i, 0))]
    for off in col_offsets:
        in_specs.append(pl.BlockSpec((k, tn), functools.partial(
            lambda i, j, ob: (0, ob + j), ob=off // tn)))
    for e, off in extras:
        in_specs.append(pl.BlockSpec((e.shape[0], tn), functools.partial(
            lambda i, j, ob: (0, ob + j), ob=off // tn)))
    out_specs = [pl.BlockSpec((tm, tn), lambda i, j: (i, j)) for _ in out_dtypes]
    out_shape = [jax.ShapeDtypeStruct((t, n_cols), dt) for dt in out_dtypes]
    return pl.pallas_call(
        functools.partial(_proj_kernel, n_w=len(col_offsets), n_extra=len(extras), epilogue=epilogue),
        grid=(t // tm, n_cols // tn),
        in_specs=in_specs,
        out_specs=out_specs,
        out_shape=out_shape,
        compiler_params=_params("parallel", "parallel"),
        name=name,
    )(h, *([w] * len(col_offsets)), *[e for e, _ in extras])


def _epi_gelu_silu(u, z):
    return (jax.nn.gelu(u) * jax.nn.silu(z),)


def _epi_gelu(v):
    return (jax.nn.gelu(v),)


def _epi_silu(q):
    return (jax.nn.silu(q),)


def _epi_identity(a):
    return (a,)


def _epi_sigmoid(a):
    return (jax.nn.sigmoid(a),)


def _epi_sigmoid_silu(og, z):
    return (jax.nn.sigmoid(og) * jax.nn.silu(z),)


def _epi_forget(f_logit, lb_logits):
    lb = jax.nn.softmax(lb_logits.astype(F32), axis=0)[0:1, :]
    f = lb + (1.0 - lb) * jax.nn.sigmoid(f_logit)
    return jnp.log(f), 1.0 - f


def _gmlp_kernel(v_ref, gate_ref, lnw_ref, lnb_ref, ws_ref, bs_ref, o_ref, wm_ref, *, n_groups):
    @pl.when(pl.program_id(0) == 0)
    def _():
        row = lax.broadcasted_iota(jnp.int32, (GMLP_BLOCK, GMLP_BLOCK), 0)
        col = lax.broadcasted_iota(jnp.int32, (GMLP_BLOCK, GMLP_BLOCK), 1)
        mask = (col // CHUNK) <= (row // CHUNK)
        for g in range(n_groups):
            wm_ref[g] = jnp.where(mask, ws_ref[g], 0.0).astype(BF16)

    v = v_ref[...].astype(F32)
    mu = jnp.mean(v, axis=-1, keepdims=True)
    xc = v - mu
    var = jnp.mean(xc * xc, axis=-1, keepdims=True)
    y = ((xc * lax.rsqrt(var + EPS)) * lnw_ref[...] + lnb_ref[...]).astype(BF16)
    n_blocks = v_ref.shape[0] // GMLP_BLOCK
    for b in range(n_blocks):
        rows = slice(b * GMLP_BLOCK, (b + 1) * GMLP_BLOCK)
        for g in range(n_groups):
            cols = slice(g * GROUP_DIM, (g + 1) * GROUP_DIM)
            mixed = jnp.dot(wm_ref[g], y[rows, cols], preferred_element_type=F32) + bs_ref[:, g:g + 1]
            o_ref[rows, cols] = (gate_ref[rows, cols].astype(F32) * mixed).astype(o_ref.dtype)


def _gmlp(gv, gate, ln_w, ln_b, w_s, b_s_t, tb=512):
    t, width = gv.shape
    n_groups = w_s.shape[0]
    return pl.pallas_call(
        functools.partial(_gmlp_kernel, n_groups=n_groups),
        grid=(t // tb,),
        in_specs=[pl.BlockSpec((tb, width), lambda i: (i, 0)),
                  pl.BlockSpec((tb, width), lambda i: (i, 0)),
                  pl.BlockSpec((1, width), lambda i: (0, 0)),
                  pl.BlockSpec((1, width), lambda i: (0, 0)),
                  pl.BlockSpec(w_s.shape, lambda i: (0, 0, 0)),
                  pl.BlockSpec(b_s_t.shape, lambda i: (0, 0))],
        out_specs=pl.BlockSpec((tb, width), lambda i: (i, 0)),
        out_shape=jax.ShapeDtypeStruct((t, width), BF16),
        scratch_shapes=[pltpu.VMEM(w_s.shape, BF16)],
        compiler_params=_params("arbitrary"),
        name="gmlp",
    )(gv, gate, ln_w, ln_b, w_s, b_s_t)


def _hgrn_kernel(q_ref, g_ref, k_ref, v_ref, gate_ref, nw_ref, o_ref, st_ref, c_ref, *, n_heads):
    ts = q_ref.shape[0]

    @pl.when(pl.program_id(1) == 0)
    def _():
        st_ref[...] = jnp.zeros_like(st_ref)

    row = lax.broadcasted_iota(jnp.int32, (ts, ts), 0)
    col = lax.broadcasted_iota(jnp.int32, (ts, ts), 1)
    tri = jnp.where(((row // CHUNK) == (col // CHUNK)) & (col <= row), 1.0, 0.0).astype(BF16)
    g = g_ref[...]
    g_hi = g.astype(BF16)
    r1 = g - g_hi.astype(F32)
    g_mid = r1.astype(BF16)
    g_lo = (r1 - g_mid.astype(F32)).astype(BF16)
    c_ref[...] = (jnp.dot(tri, g_hi, preferred_element_type=F32)
                  + jnp.dot(tri, g_mid, preferred_element_type=F32)
                  + jnp.dot(tri, g_lo, preferred_element_type=F32))

    ri = lax.broadcasted_iota(jnp.int32, (CHUNK, CHUNK), 0)
    ci = lax.broadcasted_iota(jnp.int32, (CHUNK, CHUNK), 1)
    causal = ci <= ri

    def chunk_body(n, carry):
        r0 = pl.multiple_of(n * CHUNK, CHUNK)
        rows = pl.ds(r0, CHUNK)
        for h in range(n_heads):
            cols = slice(h * GROUP_DIM, (h + 1) * GROUP_DIM)
            c = c_ref[rows, cols]
            c_mid = c[CHUNK // 2 - 1:CHUNK // 2, :]
            c_last = c[CHUNK - 1:CHUNK, :]
            d = c - c_mid
            q_in = q_ref[rows, cols].astype(F32) * jnp.exp(d)
            k_in = k_ref[rows, cols].astype(F32) * jnp.exp(-d)
            v = v_ref[rows, cols]
            scores = lax.dot_general(q_in.astype(BF16), k_in.astype(BF16),
                                     (((1,), (1,)), ((), ())), preferred_element_type=F32)
            scores = jnp.where(causal, scores, 0.0).astype(BF16)
            q_dec = (q_in * jnp.exp(c_mid)).astype(BF16)
            k_dec = (k_in * jnp.exp(c_last - c_mid)).astype(BF16)
            st = st_ref[h]
            o = (jnp.dot(scores, v, preferred_element_type=F32)
                 + lax.dot_general(q_dec, st.astype(BF16), (((1,), (1,)), ((), ())),
                                   preferred_element_type=F32))
            upd = lax.dot_general(v, k_dec, (((0,), (0,)), ((), ())), preferred_element_type=F32)
            st_ref[h] = st * jnp.exp(c_last) + upd
            ms = jnp.mean(o * o, axis=-1, keepdims=True)
            on = o * lax.rsqrt(ms + EPS) * nw_ref[:, cols]
            o_ref[rows, cols] = (on * gate_ref[rows, cols].astype(F32)).astype(o_ref.dtype)
        return carry

    lax.fori_loop(0, ts // CHUNK, chunk_body, 0)


def _hgrn(qf, g, k, v, gate, norm_w, batch, seq, ts=256):
    t, width = qf.shape
    n_heads = width // GROUP_DIM
    steps = seq // ts
    row_spec = pl.BlockSpec((ts, width), lambda b, s: (b * steps + s, 0))
    return pl.pallas_call(
        functools.partial(_hgrn_kernel, n_heads=n_heads),
        grid=(batch, steps),
        in_specs=[row_spec, row_spec, row_spec, row_spec, row_spec,
                  pl.BlockSpec((1, width), lambda b, s: (0, 0))],
        out_specs=row_spec,
        out_shape=jax.ShapeDtypeStruct((t, width), BF16),
        scratch_shapes=[pltpu.VMEM((n_heads, GROUP_DIM, GROUP_DIM), F32),
                        pltpu.VMEM((ts, width), F32)],
        compiler_params=_params("arbitrary", "arbitrary"),
        name="hgrn2",
    )(qf, g, k, v, gate, norm_w)


def _merge_kernel(ha_ref, hb_ref, wa_ref, wb_ref, ga_ref, gb_ref, o_ref):
    a = jnp.dot(ha_ref[...], wa_ref[...], preferred_element_type=F32)
    b = jnp.dot(hb_ref[...], wb_ref[...], preferred_element_type=F32)
    o_ref[...] = (ga_ref[...].astype(F32) * a + gb_ref[...].astype(F32) * b).astype(o_ref.dtype)


def _merge(h_a, h_b, w_a, w_b, gates, tm=1024, tn=512):
    t, k = h_a.shape
    d = w_a.shape[1]
    nb = d // tn
    return pl.pallas_call(
        _merge_kernel,
        grid=(t // tm, nb),
        in_specs=[pl.BlockSpec((tm, k), lambda i, j: (i, 0)),
                  pl.BlockSpec((tm, k), lambda i, j: (i, 0)),
                  pl.BlockSpec((k, tn), lambda i, j: (0, j)),
                  pl.BlockSpec((k, tn), lambda i, j: (0, j)),
                  pl.BlockSpec((tm, tn), lambda i, j: (i, j)),
                  pl.BlockSpec((tm, tn), lambda i, j: (i, nb + j))],
        out_specs=pl.BlockSpec((tm, tn), lambda i, j: (i, j)),
        out_shape=jax.ShapeDtypeStruct((t, d), BF16),
        compiler_params=_params("parallel", "parallel"),
        name="merge",
    )(h_a, h_b, w_a, w_b, gates, gates)


def _out_kernel(m_ref, w_ref, x_ref, fw_ref, o_ref, ssq_ref, *, tn):
    j = pl.program_id(1)
    y = jnp.dot(m_ref[...], w_ref[...], preferred_element_type=F32) + x_ref[...]

    @pl.when(j == 0)
    def _():
        ssq_ref[...] = jnp.zeros_like(ssq_ref)

    ssq_ref[...] += jnp.sum(y * y, axis=-1, keepdims=True)
    o_ref[:, pl.ds(pl.multiple_of(j * tn, tn), tn)] = y

    @pl.when(j == pl.num_programs(1) - 1)
    def _():
        d = o_ref.shape[1]
        scale = lax.rsqrt(ssq_ref[...] * (1.0 / d) + EPS)
        o_ref[...] = o_ref[...] * scale * fw_ref[...]


def _out_proj(merged, w_out, x2, final_w, tm=512, tn=1024):
    t, d = x2.shape
    return pl.pallas_call(
        functools.partial(_out_kernel, tn=tn),
        grid=(t // tm, d // tn),
        in_specs=[pl.BlockSpec((tm, d), lambda i, j: (i, 0)),
                  pl.BlockSpec((d, tn), lambda i, j: (0, j)),
                  pl.BlockSpec((tm, tn), lambda i, j: (i, j)),
                  pl.BlockSpec((1, d), lambda i, j: (0, 0))],
        out_specs=pl.BlockSpec((tm, d), lambda i, j: (i, 0)),
        out_shape=jax.ShapeDtypeStruct((t, d), F32),
        scratch_shapes=[pltpu.VMEM((tm, 1), F32)],
        compiler_params=_params("parallel", "arbitrary"),
        name="outproj",
    )(merged, w_out, x2, final_w)


def kernel(x, norm_w, w_in, gmlp_ln_w, gmlp_ln_b, gmlp_w_s, gmlp_b_s, hgrn_lb_logits,
           hgrn_norm_w, w_branch, w_out, final_norm_w):
    bsz, seq, d = x.shape
    assert norm_w.shape[0] == 1, "single-layer block"
    width = gmlp_ln_w.shape[1]
    t = bsz * seq
    x2 = x.reshape(t, d)

    w_in_b = w_in[0].astype(BF16)
    w_a = w_branch[0, 0].astype(BF16)
    w_b = w_branch[0, 1].astype(BF16)
    w_o = w_out[0].astype(BF16)

    h = _rmsnorm(x2, norm_w)

    off = [i * width for i in range(9)]
    (gate_a,) = _proj(h, w_in_b, [off[0], off[2]], width, _epi_gelu_silu, [BF16], tn=512, name="proj_uz")
    (gv,) = _proj(h, w_in_b, [off[1]], width, _epi_gelu, [BF16], name="proj_v")
    (qf,) = _proj(h, w_in_b, [off[3]], width, _epi_silu, [BF16], name="proj_q")
    g, k = _proj(h, w_in_b, [off[4]], width, _epi_forget, [F32, BF16],
                 extras=[(hgrn_lb_logits, 0)], name="proj_f")
    (inp,) = _proj(h, w_in_b, [off[5]], width, _epi_identity, [BF16], name="proj_inp")
    (gate_b,) = _proj(h, w_in_b, [off[6], off[7]], width, _epi_sigmoid_silu, [BF16], tn=512, name="proj_ogz")
    (gates,) = _proj(h, w_in_b, [off[8]], 2 * d, _epi_sigmoid, [BF16], name="proj_gates")

    h_a = _gmlp(gv, gate_a, gmlp_ln_w, gmlp_ln_b, gmlp_w_s[0], gmlp_b_s[0].T)
    h_b = _hgrn(qf, g, k, inp, gate_b, hgrn_norm_w, bsz, seq)

    merged = _merge(h_a, h_b, w_a, w_b, gates)
    out = _out_proj(merged, w_o, x2, final_norm_w[None, :])
    return out.reshape(bsz, seq, d)
```

```python
import functools

import jax
import jax.numpy as jnp
from jax import lax
from jax.experimental import pallas as pl
from jax.experimental.pallas import tpu as pltpu

F32 = jnp.float32
BF16 = jnp.bfloat16

EPS = 1e-6
CHUNK = 64
GMLP_BLOCK = 128
GROUP_DIM = 128
VMEM_LIMIT_BYTES = 60000 * 1024


def _params(*semantics):
    return pltpu.CompilerParams(dimension_semantics=semantics,
                                vmem_limit_bytes=VMEM_LIMIT_BYTES)


def _rmsnorm_kernel(x_ref, w_ref, o_ref):
    x = x_ref[...]
    ms = jnp.mean(x * x, axis=-1, keepdims=True)
    o_ref[...] = (x * lax.rsqrt(ms + EPS) * w_ref[...]).astype(o_ref.dtype)


def _rmsnorm(x2, w, tm=256):
    t, d = x2.shape
    return pl.pallas_call(
        _rmsnorm_kernel,
        grid=(t // tm,),
        in_specs=[pl.BlockSpec((tm, d), lambda i: (i, 0)),
                  pl.BlockSpec((1, d), lambda i: (0, 0))],
        out_specs=pl.BlockSpec((tm, d), lambda i: (i, 0)),
        out_shape=jax.ShapeDtypeStruct((t, d), BF16),
        compiler_params=_params("parallel"),
        name="prenorm",
    )(x2, w)


def _proj_kernel(*refs, n_w, n_extra, n_out, epilogue):
    a_ref = refs[0]
    w_refs = refs[1:1 + n_w]
    extra_refs = refs[1 + n_w:1 + n_w + n_extra]
    rest = refs[1 + n_w + n_extra:]
    a = a_ref[...]
    accs = [jnp.dot(a, w[...], preferred_element_type=F32) for w in w_refs]
    outs = epilogue(*accs, *[e[...] for e in extra_refs])
    if len(rest) > n_out:
        cast_in_ref, out_refs, cast_out_ref = rest[0], rest[1:1 + n_out], rest[1 + n_out]
        cast_out_ref[...] = cast_in_ref[...].astype(cast_out_ref.dtype)
    else:
        out_refs = rest
    for o_ref, o in zip(out_refs, outs):
        o_ref[...] = o.astype(o_ref.dtype)


def _proj(h, w, col_offsets, n_cols, epilogue, out_dtypes, extras=(), cast=None, tm=1024, tn=1024,
          name="proj"):
    t, k = h.shape
    ni, nj = t // tm, n_cols // tn
    in_specs = [pl.BlockSpec((tm, k), lambda i, j: (i, 0))]
    for off in col_offsets:
        in_specs.append(pl.BlockSpec((k, tn), functools.partial(
            lambda i, j, ob: (0, ob + j), ob=off // tn)))
    for e, off in extras:
        in_specs.append(pl.BlockSpec((e.shape[0], tn), functools.partial(
            lambda i, j, ob: (0, ob + j), ob=off // tn)))
    out_specs = [pl.BlockSpec((tm, tn), lambda i, j: (i, j)) for _ in out_dtypes]
    out_shape = [jax.ShapeDtypeStruct((t, n_cols), dt) for dt in out_dtypes]
    operands = [h, *([w] * len(col_offsets)), *[e for e, _ in extras]]
    if cast is not None:
        src, axis, start, length = cast
        steps = ni * nj
        size = length // steps
        assert size * steps == length and start % size == 0
        first = start // size
        if axis == 0:
            blk = (size, src.shape[1])
            in_specs.append(pl.BlockSpec(blk, lambda i, j: (first + i * nj + j, 0)))
            out_specs.append(pl.BlockSpec(blk, lambda i, j: (i * nj + j, 0)))
            out_shape.append(jax.ShapeDtypeStruct((length, src.shape[1]), BF16))
        else:
            blk = (src.shape[0], size)
            in_specs.append(pl.BlockSpec(blk, lambda i, j: (0, first + i * nj + j)))
            out_specs.append(pl.BlockSpec(blk, lambda i, j: (0, i * nj + j)))
            out_shape.append(jax.ShapeDtypeStruct((src.shape[0], length), BF16))
        operands.append(src)
    return pl.pallas_call(
        functools.partial(_proj_kernel, n_w=len(col_offsets), n_extra=len(extras),
                          n_out=len(out_dtypes), epilogue=epilogue),
        grid=(ni, nj),
        in_specs=in_specs,
        out_specs=out_specs,
        out_shape=out_shape,
        compiler_params=_params("parallel", "parallel"),
        name=name,
    )(*operands)


def _sigmoid(a):
    return 0.5 * jnp.tanh(0.5 * a) + 0.5


def _silu(a):
    return a * _sigmoid(a)


def _epi_gelu_silu(u, z):
    return (jax.nn.gelu(u) * _silu(z),)


def _epi_gelu(v):
    return (jax.nn.gelu(v),)


def _epi_silu(q):
    return (_silu(q),)


def _epi_identity(a):
    return (a,)


def _epi_sigmoid(a):
    return (_sigmoid(a),)


def _epi_sigmoid_silu_gain(og, z, gain):
    return (_sigmoid(og) * _silu(z) * gain,)


def _epi_forget(f_logit, lb_logits):
    lb = jax.nn.softmax(lb_logits.astype(F32), axis=0)[0:1, :]
    f = lb + (1.0 - lb) * _sigmoid(f_logit)
    g = jnp.log2(f)
    g_hi = g.astype(BF16)
    return g_hi, g - g_hi.astype(F32), 1.0 - f


def _gmlp_kernel(v_ref, gate_ref, lnw_ref, lnb_ref, ws_ref, bs_ref, o_ref, wm_ref, *, n_groups):
    @pl.when(pl.program_id(0) == 0)
    def _():
        row = lax.broadcasted_iota(jnp.int32, (GMLP_BLOCK, GMLP_BLOCK), 0)
        col = lax.broadcasted_iota(jnp.int32, (GMLP_BLOCK, GMLP_BLOCK), 1)
        mask = (col // CHUNK) <= (row // CHUNK)
        for g in range(n_groups):
            wm_ref[g] = jnp.where(mask, ws_ref[g], 0.0).astype(BF16)

    v = v_ref[...].astype(F32)
    mu = jnp.mean(v, axis=-1, keepdims=True)
    xc = v - mu
    var = jnp.mean(xc * xc, axis=-1, keepdims=True)
    y = ((xc * lax.rsqrt(var + EPS)) * lnw_ref[...] + lnb_ref[...]).astype(BF16)
    n_blocks = v_ref.shape[0] // GMLP_BLOCK
    for b in range(n_blocks):
        rows = slice(b * GMLP_BLOCK, (b + 1) * GMLP_BLOCK)
        for g in range(n_groups):
            cols = slice(g * GROUP_DIM, (g + 1) * GROUP_DIM)
            mixed = jnp.dot(wm_ref[g], y[rows, cols], preferred_element_type=F32) + bs_ref[:, g:g + 1]
            o_ref[rows, cols] = (gate_ref[rows, cols].astype(F32) * mixed).astype(o_ref.dtype)


def _gmlp(gv, gate, ln_w, ln_b, w_s, b_s_t, tb=512):
    t, width = gv.shape
    n_groups = w_s.shape[0]
    return pl.pallas_call(
        functools.partial(_gmlp_kernel, n_groups=n_groups),
        grid=(t // tb,),
        in_specs=[pl.BlockSpec((tb, width), lambda i: (i, 0)),
                  pl.BlockSpec((tb, width), lambda i: (i, 0)),
                  pl.BlockSpec((1, width), lambda i: (0, 0)),
                  pl.BlockSpec((1, width), lambda i: (0, 0)),
                  pl.BlockSpec(w_s.shape, lambda i: (0, 0, 0)),
                  pl.BlockSpec(b_s_t.shape, lambda i: (0, 0))],
        out_specs=pl.BlockSpec((tb, width), lambda i: (i, 0)),
        out_shape=jax.ShapeDtypeStruct((t, width), BF16),
        scratch_shapes=[pltpu.VMEM(w_s.shape, BF16)],
        compiler_params=_params("arbitrary"),
        name="gmlp",
    )(gv, gate, ln_w, ln_b, w_s, b_s_t)


def _hgrn_kernel(q_ref, ghi_ref, glo_ref, k_ref, v_ref, gate_ref, o_ref,
                 st_ref, qin_ref, kin_ref, qdec_ref, kdec_ref, dec_ref, p_ref, oi_ref, sbf_ref,
                 *, n_heads):
    ts = q_ref.shape[0]
    nc = ts // CHUNK

    @pl.when(pl.program_id(1) == 0)
    def _():
        st_ref[...] = jnp.zeros_like(st_ref)

    row = lax.broadcasted_iota(jnp.int32, (ts, ts), 0)
    col = lax.broadcasted_iota(jnp.int32, (ts, ts), 1)
    tri = jnp.where(((row // CHUNK) == (col // CHUNK)) & (col <= row), 1.0, 0.0).astype(BF16)
    gw = GROUP_DIM * 4
    for grp in range(q_ref.shape[1] // gw):
        gc = slice(grp * gw, (grp + 1) * gw)
        c_all = (jnp.dot(tri, ghi_ref[:, gc], preferred_element_type=F32)
                 + jnp.dot(tri, glo_ref[:, gc], preferred_element_type=F32))
        for n in range(nc):
            rows = slice(n * CHUNK, (n + 1) * CHUNK)
            c = c_all[n * CHUNK:(n + 1) * CHUNK]
            c_mid = c[CHUNK // 2 - 1:CHUNK // 2, :]
            c_last = c[CHUNK - 1:CHUNK, :]
            q_in = q_ref[rows, gc] * jnp.exp2(c - c_mid).astype(BF16)
            k_in = k_ref[rows, gc] * jnp.exp2(c_mid - c).astype(BF16)
            qin_ref[rows, gc] = q_in
            kin_ref[rows, gc] = k_in
            qdec_ref[rows, gc] = q_in * jnp.exp2(c_mid).astype(BF16)
            kdec_ref[rows, gc] = k_in * jnp.exp2(c_last - c_mid).astype(BF16)
            dec_ref[n:n + 1, gc] = jnp.exp2(c_last)

    ri = lax.broadcasted_iota(jnp.int32, (CHUNK, CHUNK), 0)
    ci = lax.broadcasted_iota(jnp.int32, (CHUNK, CHUNK), 1)
    causal = ci <= ri

    for n in range(nc):
        rows = slice(n * CHUNK, (n + 1) * CHUNK)
        for h in range(n_heads):
            cols = slice(h * GROUP_DIM, (h + 1) * GROUP_DIM)
            scores = lax.dot_general(qin_ref[rows, cols], kin_ref[rows, cols],
                                     (((1,), (1,)), ((), ())), preferred_element_type=F32)
            p_ref[h, rows, :] = jnp.where(causal, scores, 0.0).astype(BF16)

    for n in range(nc):
        rows = slice(n * CHUNK, (n + 1) * CHUNK)
        for h in range(n_heads):
            cols = slice(h * GROUP_DIM, (h + 1) * GROUP_DIM)
            v = v_ref[rows, cols]
            oi_ref[rows, cols] = jnp.dot(p_ref[h, rows, :], v, preferred_element_type=F32)
            upd = lax.dot_general(v, kdec_ref[rows, cols], (((0,), (0,)), ((), ())),
                                  preferred_element_type=F32)
            st = st_ref[h]
            sbf_ref[n, h] = st.astype(BF16).T
            st_ref[h] = st * dec_ref[n:n + 1, cols] + upd

    for n in range(nc):
        rows = slice(n * CHUNK, (n + 1) * CHUNK)
        for h in range(n_heads):
            cols = slice(h * GROUP_DIM, (h + 1) * GROUP_DIM)
            o = oi_ref[rows, cols] + jnp.dot(qdec_ref[rows, cols], sbf_ref[n, h],
                                             preferred_element_type=F32)
            ms = jnp.mean(o * o, axis=-1, keepdims=True)
            on = o * lax.rsqrt(ms + EPS)
            o_ref[rows, cols] = (on * gate_ref[rows, cols].astype(F32)).astype(o_ref.dtype)


def _hgrn(qf, g_hi, g_lo, k, v, gate, batch, seq, ts=256):
    t, width = qf.shape
    n_heads = width // GROUP_DIM
    steps = seq // ts
    nc = ts // CHUNK
    row_spec = pl.BlockSpec((ts, width), lambda b, s: (b * steps + s, 0))
    return pl.pallas_call(
        functools.partial(_hgrn_kernel, n_heads=n_heads),
        grid=(batch, steps),
        in_specs=[row_spec] * 6,
        out_specs=row_spec,
        out_shape=jax.ShapeDtypeStruct((t, width), BF16),
        scratch_shapes=[pltpu.VMEM((n_heads, GROUP_DIM, GROUP_DIM), F32),
                        pltpu.VMEM((ts, width), BF16), pltpu.VMEM((ts, width), BF16),
                        pltpu.VMEM((ts, width), BF16), pltpu.VMEM((ts, width), BF16),
                        pltpu.VMEM((8, width), F32),
                        pltpu.VMEM((n_heads, ts, CHUNK), BF16),
                        pltpu.VMEM((ts, width), F32),
                        pltpu.VMEM((nc, n_heads, GROUP_DIM, GROUP_DIM), BF16)],
        compiler_params=_params("arbitrary", "arbitrary"),
        name="hgrn2",
    )(qf, g_hi, g_lo, k, v, gate)


def _merge_kernel(ha_ref, hb_ref, wa_ref, wb_ref, ga_ref, gb_ref, o_ref):
    a = jnp.dot(ha_ref[...], wa_ref[...], preferred_element_type=F32)
    b = jnp.dot(hb_ref[...], wb_ref[...], preferred_element_type=F32)
    o_ref[...] = (ga_ref[...].astype(F32) * a + gb_ref[...].astype(F32) * b).astype(o_ref.dtype)


def _merge(h_a, h_b, w_ab, gates, tm=1024, tn=512):
    t, k = h_a.shape
    d = w_ab.shape[1]
    nb = d // tn
    return pl.pallas_call(
        _merge_kernel,
        grid=(t // tm, nb),
        in_specs=[pl.BlockSpec((tm, k), lambda i, j: (i, 0)),
                  pl.BlockSpec((tm, k), lambda i, j: (i, 0)),
                  pl.BlockSpec((k, tn), lambda i, j: (0, j)),
                  pl.BlockSpec((k, tn), lambda i, j: (1, j)),
                  pl.BlockSpec((tm, tn), lambda i, j: (i, j)),
                  pl.BlockSpec((tm, tn), lambda i, j: (i, nb + j))],
        out_specs=pl.BlockSpec((tm, tn), lambda i, j: (i, j)),
        out_shape=jax.ShapeDtypeStruct((t, d), BF16),
        compiler_params=_params("parallel", "parallel"),
        name="merge",
    )(h_a, h_b, w_ab, w_ab, gates, gates)


def _out_kernel(m_ref, w_ref, x_ref, fw_ref, o_ref, ssq_ref, *, tn):
    j = pl.program_id(1)
    y = jnp.dot(m_ref[...], w_ref[...], preferred_element_type=F32) + x_ref[...]

    @pl.when(j == 0)
    def _():
        ssq_ref[...] = jnp.zeros_like(ssq_ref)

    ssq_ref[...] += jnp.sum(y * y, axis=-1, keepdims=True)
    o_ref[:, pl.ds(pl.multiple_of(j * tn, tn), tn)] = y

    @pl.when(j == pl.num_programs(1) - 1)
    def _():
        d = o_ref.shape[1]
        scale = lax.rsqrt(ssq_ref[...] * (1.0 / d) + EPS)
        o_ref[...] = o_ref[...] * scale * fw_ref[...]


def _out_proj(merged, w_out, x2, final_w, tm=512, tn=1024):
    t, d = x2.shape
    return pl.pallas_call(
        functools.partial(_out_kernel, tn=tn),
        grid=(t // tm, d // tn),
        in_specs=[pl.BlockSpec((tm, d), lambda i, j: (i, 0)),
                  pl.BlockSpec((d, tn), lambda i, j: (0, j)),
                  pl.BlockSpec((tm, tn), lambda i, j: (i, j)),
                  pl.BlockSpec((1, d), lambda i, j: (0, 0))],
        out_specs=pl.BlockSpec((tm, d), lambda i, j: (i, 0)),
        out_shape=jax.ShapeDtypeStruct((t, d), F32),
        scratch_shapes=[pltpu.VMEM((tm, 1), F32)],
        compiler_params=_params("parallel", "arbitrary"),
        name="outproj",
    )(merged, w_out, x2, final_w)


def kernel(x, norm_w, w_in, gmlp_ln_w, gmlp_ln_b, gmlp_w_s, gmlp_b_s, hgrn_lb_logits,
           hgrn_norm_w, w_branch, w_out, final_norm_w):
    bsz, seq, d = x.shape
    assert norm_w.shape[0] == 1, "single-layer block"
    width = gmlp_ln_w.shape[1]
    t = bsz * seq
    x2 = x.reshape(t, d)

    h = _rmsnorm(x2, norm_w)

    off = [i * width for i in range(9)]
    w_in2 = w_in[0]
    w_inp = w_in2[:, off[5]:off[6]].astype(BF16)
    inp, w_gates = _proj(h, w_inp, [0], width, _epi_identity, [BF16],
                         cast=(w_in2, 1, off[8], 2 * d), name="proj_inp")
    gates, w_rest = _proj(h, w_gates, [0], 2 * d, _epi_sigmoid, [BF16],
                          cast=(w_in2, 1, 0, off[8]), name="proj_gates")
    gate_a, w_o = _proj(h, w_rest, [off[0], off[2]], width, _epi_gelu_silu, [BF16],
                        cast=(w_out[0], 0, 0, d), tn=512, name="proj_uz")
    gate_b, w_ab = _proj(h, w_rest, [off[6], off[7]], width, _epi_sigmoid_silu_gain, [BF16],
                         extras=[(hgrn_norm_w, 0)],
                         cast=(w_branch[0].reshape(2 * width, d), 0, 0, 2 * width), tn=512, name="proj_ogz")
    (gv,) = _proj(h, w_rest, [off[1]], width, _epi_gelu, [BF16], name="proj_v")
    (qf,) = _proj(h, w_rest, [off[3]], width, _epi_silu, [BF16], name="proj_q")
    g_hi, g_lo, k = _proj(h, w_rest, [off[4]], width, _epi_forget, [BF16, BF16, BF16],
                          extras=[(hgrn_lb_logits, 0)], name="proj_f")

    h_a = _gmlp(gv, gate_a, gmlp_ln_w, gmlp_ln_b, gmlp_w_s[0], gmlp_b_s[0].T)
    h_b = _hgrn(qf, g_hi, g_lo, k, inp, gate_b, bsz, seq)

    merged = _merge(h_a, h_b, w_ab, gates)
    out = _out_proj(merged, w_o, x2, final_norm_w[None, :])
    return out.reshape(bsz, seq, d)
```

```python
import functools

import jax
import jax.numpy as jnp
from jax import lax
from jax.experimental import pallas as pl
from jax.experimental.pallas import tpu as pltpu

F32 = jnp.float32
BF16 = jnp.bfloat16

EPS = 1e-6
CHUNK = 64
GMLP_BLOCK = 128
GROUP_DIM = 128
VMEM_LIMIT_BYTES = 60000 * 1024


def _params(*semantics):
    return pltpu.CompilerParams(dimension_semantics=semantics,
                                vmem_limit_bytes=VMEM_LIMIT_BYTES)


def _first_kernel(x_ref, nw_ref, w_ref, cast_in_ref, h_ref, rs_ref, o_ref, cast_out_ref):
    x = x_ref[...]
    xw = (x * nw_ref[...]).astype(BF16)
    h_ref[...] = xw
    rs = lax.rsqrt(jnp.mean(x * x, axis=-1, keepdims=True) + EPS)
    rs_ref[...] = jnp.broadcast_to(rs, rs_ref.shape)
    o_ref[...] = (jnp.dot(xw, w_ref[...], preferred_element_type=F32) * rs).astype(o_ref.dtype)
    cast_out_ref[...] = cast_in_ref[...].astype(cast_out_ref.dtype)


def _first(x2, nw, w, cast_src, cast_start, cast_len, tm=256):
    t, d = x2.shape
    n = w.shape[1]
    steps = t // tm
    cw = cast_len // steps
    assert cw * steps == cast_len and cast_start % cw == 0
    first = cast_start // cw
    return pl.pallas_call(
        _first_kernel,
        grid=(steps,),
        in_specs=[pl.BlockSpec((tm, d), lambda i: (i, 0)),
                  pl.BlockSpec((1, d), lambda i: (0, 0)),
                  pl.BlockSpec((d, n), lambda i: (0, 0), pipeline_mode=pl.Buffered(1)),
                  pl.BlockSpec((cast_src.shape[0], cw), lambda i: (0, first + i))],
        out_specs=[pl.BlockSpec((tm, d), lambda i: (i, 0)),
                   pl.BlockSpec((tm, GROUP_DIM), lambda i: (i, 0)),
                   pl.BlockSpec((tm, n), lambda i: (i, 0)),
                   pl.BlockSpec((cast_src.shape[0], cw), lambda i: (0, i))],
        out_shape=[jax.ShapeDtypeStruct((t, d), BF16),
                   jax.ShapeDtypeStruct((t, GROUP_DIM), F32),
                   jax.ShapeDtypeStruct((t, n), BF16),
                   jax.ShapeDtypeStruct((cast_src.shape[0], cast_len), BF16)],
        compiler_params=_params("parallel"),
        name="prenorm_proj_inp",
    )(x2, nw, w, cast_src)


def _proj_kernel(*refs, n_w, n_extra, n_out, epilogue):
    a_ref, rs_ref = refs[0], refs[1]
    w_refs = refs[2:2 + n_w]
    extra_refs = refs[2 + n_w:2 + n_w + n_extra]
    rest = refs[2 + n_w + n_extra:]
    a = a_ref[...]
    rs = rs_ref[...]
    rs = jnp.concatenate([rs] * (w_refs[0].shape[1] // rs.shape[1]), axis=1)
    accs = [jnp.dot(a, w[...], preferred_element_type=F32) * rs for w in w_refs]
    outs = epilogue(*accs, *[e[...] for e in extra_refs])
    if len(rest) > n_out:
        cast_in_ref, out_refs, cast_out_ref = rest[0], rest[1:1 + n_out], rest[1 + n_out]
        cast_out_ref[...] = cast_in_ref[...].astype(cast_out_ref.dtype)
    else:
        out_refs = rest
    for o_ref, o in zip(out_refs, outs):
        o_ref[...] = o.astype(o_ref.dtype)


def _proj(h, rs, w, col_offsets, n_cols, epilogue, out_dtypes, extras=(), cast=None, tm=1024, tn=1024,
          name="proj"):
    t, k = h.shape
    ni, nj = t // tm, n_cols // tn
    in_specs = [pl.BlockSpec((tm, k), lambda i, j: (i, 0)),
                pl.BlockSpec((tm, rs.shape[1]), lambda i, j: (i, 0))]
    for off in col_offsets:
        in_specs.append(pl.BlockSpec((k, tn), functools.partial(
            lambda i, j, ob: (0, ob + j), ob=off // tn)))
    for e, off in extras:
        in_specs.append(pl.BlockSpec((e.shape[0], tn), functools.partial(
            lambda i, j, ob: (0, ob + j), ob=off // tn)))
    out_specs = [pl.BlockSpec((tm, tn), lambda i, j: (i, j)) for _ in out_dtypes]
    out_shape = [jax.ShapeDtypeStruct((t, n_cols), dt) for dt in out_dtypes]
    operands = [h, rs, *([w] * len(col_offsets)), *[e for e, _ in extras]]
    if cast is not None:
        src, axis, start, length = cast
        steps = ni * nj
        size = length // steps
        assert size * steps == length and start % size == 0
        first = start // size
        if axis == 0:
            blk = (size, src.shape[1])
            in_specs.append(pl.BlockSpec(blk, lambda i, j: (first + i * nj + j, 0)))
            out_specs.append(pl.BlockSpec(blk, lambda i, j: (i * nj + j, 0)))
            out_shape.append(jax.ShapeDtypeStruct((length, src.shape[1]), BF16))
        else:
            blk = (src.shape[0], size)
            in_specs.append(pl.BlockSpec(blk, lambda i, j: (0, first + i * nj + j)))
            out_specs.append(pl.BlockSpec(blk, lambda i, j: (0, i * nj + j)))
            out_shape.append(jax.ShapeDtypeStruct((src.shape[0], length), BF16))
        operands.append(src)
    return pl.pallas_call(
        functools.partial(_proj_kernel, n_w=len(col_offsets), n_extra=len(extras),
                          n_out=len(out_dtypes), epilogue=epilogue),
        grid=(ni, nj),
        in_specs=in_specs,
        out_specs=out_specs,
        out_shape=out_shape,
        compiler_params=_params("parallel", "parallel"),
        name=name,
    )(*operands)


def _sigmoid(a):
    return 0.5 * jnp.tanh(0.5 * a) + 0.5


def _silu(a):
    return a * _sigmoid(a)


def _epi_gelu_silu(u, z):
    return (jax.nn.gelu(u) * _silu(z),)


def _epi_gelu(v):
    return (jax.nn.gelu(v),)


def _epi_silu(q):
    return (_silu(q),)


def _epi_sigmoid(a):
    return (_sigmoid(a),)


def _epi_sigmoid_silu_gain(og, z, gain):
    return (_sigmoid(og) * _silu(z) * gain,)


def _epi_forget(f_logit, lb_logits):
    lb = jax.nn.softmax(lb_logits.astype(F32), axis=0)[0:1, :]
    f = lb + (1.0 - lb) * _sigmoid(f_logit)
    g = jnp.log2(f)
    g_hi = g.astype(BF16)
    return g_hi, g - g_hi.astype(F32), 1.0 - f


def _gmlp_kernel(v_ref, gate_ref, lnw_ref, lnb_ref, ws_ref, bs_ref, o_ref, wm_ref, *, n_groups):
    @pl.when(pl.program_id(0) == 0)
    def _():
        row = lax.broadcasted_iota(jnp.int32, (GMLP_BLOCK, GMLP_BLOCK), 0)
        col = lax.broadcasted_iota(jnp.int32, (GMLP_BLOCK, GMLP_BLOCK), 1)
        mask = (col // CHUNK) <= (row // CHUNK)
        for g in range(n_groups):
            wm_ref[g] = jnp.where(mask, ws_ref[g], 0.0).astype(BF16)

    v = v_ref[...].astype(F32)
    mu = jnp.mean(v, axis=-1, keepdims=True)
    xc = v - mu
    var = jnp.mean(xc * xc, axis=-1, keepdims=True)
    y = ((xc * lax.rsqrt(var + EPS)) * lnw_ref[...] + lnb_ref[...]).astype(BF16)
    n_blocks = v_ref.shape[0] // GMLP_BLOCK
    for b in range(n_blocks):
        rows = slice(b * GMLP_BLOCK, (b + 1) * GMLP_BLOCK)
        for g in range(n_groups):
            cols = slice(g * GROUP_DIM, (g + 1) * GROUP_DIM)
            mixed = jnp.dot(wm_ref[g], y[rows, cols], preferred_element_type=F32) + bs_ref[:, g:g + 1]
            o_ref[rows, cols] = (gate_ref[rows, cols].astype(F32) * mixed).astype(o_ref.dtype)


def _gmlp(gv, gate, ln_w, ln_b, w_s, b_s_t, tb=512):
    t, width = gv.shape
    n_groups = w_s.shape[0]
    return pl.pallas_call(
        functools.partial(_gmlp_kernel, n_groups=n_groups),
        grid=(t // tb,),
        in_specs=[pl.BlockSpec((tb, width), lambda i: (i, 0)),
                  pl.BlockSpec((tb, width), lambda i: (i, 0)),
                  pl.BlockSpec((1, width), lambda i: (0, 0)),
                  pl.BlockSpec((1, width), lambda i: (0, 0)),
                  pl.BlockSpec(w_s.shape, lambda i: (0, 0, 0)),
                  pl.BlockSpec(b_s_t.shape, lambda i: (0, 0))],
        out_specs=pl.BlockSpec((tb, width), lambda i: (i, 0)),
        out_shape=jax.ShapeDtypeStruct((t, width), BF16),
        scratch_shapes=[pltpu.VMEM(w_s.shape, BF16)],
        compiler_params=_params("arbitrary"),
        name="gmlp",
    )(gv, gate, ln_w, ln_b, w_s, b_s_t)


def _hgrn_kernel(q_ref, ghi_ref, glo_ref, k_ref, v_ref, gate_ref, o_ref,
                 st_ref, qin_ref, kin_ref, qdec_ref, kdec_ref, dec_ref, p_ref, oi_ref, sbf_ref,
                 *, n_heads):
    ts = q_ref.shape[0]
    nc = ts // CHUNK

    @pl.when(pl.program_id(1) == 0)
    def _():
        st_ref[...] = jnp.zeros_like(st_ref)

    row = lax.broadcasted_iota(jnp.int32, (ts, ts), 0)
    col = lax.broadcasted_iota(jnp.int32, (ts, ts), 1)
    tri = jnp.where(((row // CHUNK) == (col // CHUNK)) & (col <= row), 1.0, 0.0).astype(BF16)
    gw = GROUP_DIM * 4
    for grp in range(q_ref.shape[1] // gw):
        gc = slice(grp * gw, (grp + 1) * gw)
        c_all = (jnp.dot(tri, ghi_ref[:, gc], preferred_element_type=F32)
                 + jnp.dot(tri, glo_ref[:, gc], preferred_element_type=F32))
        for n in range(nc):
            rows = slice(n * CHUNK, (n + 1) * CHUNK)
            c = c_all[n * CHUNK:(n + 1) * CHUNK]
            c_mid = c[CHUNK // 2 - 1:CHUNK // 2, :]
            c_last = c[CHUNK - 1:CHUNK, :]
            q_in = q_ref[rows, gc] * jnp.exp2(c - c_mid).astype(BF16)
            k_in = k_ref[rows, gc] * jnp.exp2(c_mid - c).astype(BF16)
            qin_ref[rows, gc] = q_in
            kin_ref[rows, gc] = k_in
            qdec_ref[rows, gc] = q_in * jnp.exp2(c_mid).astype(BF16)
            kdec_ref[rows, gc] = k_in * jnp.exp2(c_last - c_mid).astype(BF16)
            dec_ref[n:n + 1, gc] = jnp.exp2(c_last)

    ri = lax.broadcasted_iota(jnp.int32, (CHUNK, CHUNK), 0)
    ci = lax.broadcasted_iota(jnp.int32, (CHUNK, CHUNK), 1)
    causal = ci <= ri

    for n in range(nc):
        rows = slice(n * CHUNK, (n + 1) * CHUNK)
        for h in range(n_heads):
            cols = slice(h * GROUP_DIM, (h + 1) * GROUP_DIM)
            scores = lax.dot_general(qin_ref[rows, cols], kin_ref[rows, cols],
                                     (((1,), (1,)), ((), ())), preferred_element_type=F32)
            p_ref[h, rows, :] = jnp.where(causal, scores, 0.0).astype(BF16)

    for n in range(nc):
        rows = slice(n * CHUNK, (n + 1) * CHUNK)
        for h in range(n_heads):
            cols = slice(h * GROUP_DIM, (h + 1) * GROUP_DIM)
            v = v_ref[rows, cols]
            oi_ref[rows, cols] = jnp.dot(p_ref[h, rows, :], v, preferred_element_type=F32)
            upd = lax.dot_general(v, kdec_ref[rows, cols], (((0,), (0,)), ((), ())),
                                  preferred_element_type=F32)
            st = st_ref[h]
            sbf_ref[n, h] = st.astype(BF16).T
            st_ref[h] = st * dec_ref[n:n + 1, cols] + upd

    for n in range(nc):
        rows = slice(n * CHUNK, (n + 1) * CHUNK)
        for h in range(n_heads):
            cols = slice(h * GROUP_DIM, (h + 1) * GROUP_DIM)
            o = oi_ref[rows, cols] + jnp.dot(qdec_ref[rows, cols], sbf_ref[n, h],
                                             preferred_element_type=F32)
            ms = jnp.mean(o * o, axis=-1, keepdims=True)
            on = o * lax.rsqrt(ms + EPS)
            o_ref[rows, cols] = (on * gate_ref[rows, cols].astype(F32)).astype(o_ref.dtype)


def _hgrn(qf, g_hi, g_lo, k, v, gate, batch, seq, ts=256):
    t, width = qf.shape
    n_heads = width // GROUP_DIM
    steps = seq // ts
    nc = ts // CHUNK
    row_spec = pl.BlockSpec((ts, width), lambda b, s: (b * steps + s, 0))
    return pl.pallas_call(
        functools.partial(_hgrn_kernel, n_heads=n_heads),
        grid=(batch, steps),
        in_specs=[row_spec] * 6,
        out_specs=row_spec,
        out_shape=jax.ShapeDtypeStruct((t, width), BF16),
        scratch_shapes=[pltpu.VMEM((n_heads, GROUP_DIM, GROUP_DIM), F32),
                        pltpu.VMEM((ts, width), BF16), pltpu.VMEM((ts, width), BF16),
                        pltpu.VMEM((ts, width), BF16), pltpu.VMEM((ts, width), BF16),
                        pltpu.VMEM((8, width), F32),
                        pltpu.VMEM((n_heads, ts, CHUNK), BF16),
                        pltpu.VMEM((ts, width), F32),
                        pltpu.VMEM((nc, n_heads, GROUP_DIM, GROUP_DIM), BF16)],
        compiler_params=_params("arbitrary", "arbitrary"),
        name="hgrn2",
    )(qf, g_hi, g_lo, k, v, gate)


def _merge_kernel(ha_ref, hb_ref, wa_ref, wb_ref, ga_ref, gb_ref, o_ref):
    a = jnp.dot(ha_ref[...], wa_ref[...], preferred_element_type=F32)
    b = jnp.dot(hb_ref[...], wb_ref[...], preferred_element_type=F32)
    o_ref[...] = (ga_ref[...].astype(F32) * a + gb_ref[...].astype(F32) * b).astype(o_ref.dtype)


def _merge(h_a, h_b, w_ab, gates, tm=256):
    t, k = h_a.shape
    d = w_ab.shape[1]
    return pl.pallas_call(
        _merge_kernel,
        grid=(t // tm,),
        in_specs=[pl.BlockSpec((tm, k), lambda i: (i, 0)),
                  pl.BlockSpec((tm, k), lambda i: (i, 0)),
                  pl.BlockSpec((k, d), lambda i: (0, 0), pipeline_mode=pl.Buffered(1)),
                  pl.BlockSpec((k, d), lambda i: (1, 0), pipeline_mode=pl.Buffered(1)),
                  pl.BlockSpec((tm, d), lambda i: (i, 0)),
                  pl.BlockSpec((tm, d), lambda i: (i, 1))],
        out_specs=pl.BlockSpec((tm, d), lambda i: (i, 0)),
        out_shape=jax.ShapeDtypeStruct((t, d), BF16),
        compiler_params=_params("parallel"),
        name="merge",
    )(h_a, h_b, w_ab, w_ab, gates, gates)


def _out_kernel(m_ref, w_ref, x_ref, fw_ref, o_ref):
    y = jnp.dot(m_ref[...], w_ref[...], preferred_element_type=F32) + x_ref[...]
    ms = jnp.mean(y * y, axis=-1, keepdims=True)
    o_ref[...] = y * lax.rsqrt(ms + EPS) * fw_ref[...]


def _out_proj(merged, w_out, x2, final_w, tm=256):
    t, d = x2.shape
    return pl.pallas_call(
        _out_kernel,
        grid=(t // tm,),
        in_specs=[pl.BlockSpec((tm, d), lambda i: (i, 0)),
                  pl.BlockSpec((d, d), lambda i: (0, 0), pipeline_mode=pl.Buffered(1)),
                  pl.BlockSpec((tm, d), lambda i: (i, 0)),
                  pl.BlockSpec((1, d), lambda i: (0, 0))],
        out_specs=pl.BlockSpec((tm, d), lambda i: (i, 0)),
        out_shape=jax.ShapeDtypeStruct((t, d), F32),
        compiler_params=_params("parallel"),
        name="outproj",
    )(merged, w_out, x2, final_w)


def kernel(x, norm_w, w_in, gmlp_ln_w, gmlp_ln_b, gmlp_w_s, gmlp_b_s, hgrn_lb_logits,
           hgrn_norm_w, w_branch, w_out, final_norm_w):
    bsz, seq, d = x.shape
    assert norm_w.shape[0] == 1, "single-layer block"
    width = gmlp_ln_w.shape[1]
    t = bsz * seq
    x2 = x.reshape(t, d)

    off = [i * width for i in range(9)]
    w_in2 = w_in[0]
    w_inp = w_in2[:, off[5]:off[6]].astype(BF16)
    h, rs, inp, w_gates = _first(x2, norm_w, w_inp, w_in2, off[8], 2 * d)
    gates, w_rest = _proj(h, rs, w_gates, [0], 2 * d, _epi_sigmoid, [BF16],
                          cast=(w_in2, 1, 0, off[8]), name="proj_gates")
    gate_a, w_o = _proj(h, rs, w_rest, [off[0], off[2]], width, _epi_gelu_silu, [BF16],
                        cast=(w_out[0], 0, 0, d), tn=512, name="proj_uz")
    gate_b, w_ab = _proj(h, rs, w_rest, [off[6], off[7]], width, _epi_sigmoid_silu_gain, [BF16],
                         extras=[(hgrn_norm_w, 0)],
                         cast=(w_branch[0].reshape(2 * width, d), 0, 0, 2 * width), tn=512, name="proj_ogz")
    (gv,) = _proj(h, rs, w_rest, [off[1]], width, _epi_gelu, [BF16], name="proj_v")
    (qf,) = _proj(h, rs, w_rest, [off[3]], width, _epi_silu, [BF16], name="proj_q")
    g_hi, g_lo, k = _proj(h, rs, w_rest, [off[4]], width, _epi_forget, [BF16, BF16, BF16],
                          extras=[(hgrn_lb_logits, 0)], name="proj_f")

    h_a = _gmlp(gv, gate_a, gmlp_ln_w, gmlp_ln_b, gmlp_w_s[0], gmlp_b_s[0].T)
    h_b = _hgrn(qf, g_hi, g_lo, k, inp, gate_b, bsz, seq)

    merged = _merge(h_a, h_b, w_ab, gates)
    out = _out_proj(merged, w_o, x2, final_norm_w[None, :])
    return out.reshape(bsz, seq, d)
```

```python
import functools

import jax
import jax.numpy as jnp
from jax import lax
from jax.experimental import pallas as pl
from jax.experimental.pallas import tpu as pltpu

F32 = jnp.float32
BF16 = jnp.bfloat16

EPS = 1e-6
CHUNK = 64
GMLP_BLOCK = 128
GROUP_DIM = 128
VMEM_LIMIT_BYTES = 60000 * 1024
VMEM_LIMIT_3DOT_BYTES = 62 * 1024 * 1024


def _params(*semantics, vmem_limit=VMEM_LIMIT_BYTES):
    return pltpu.CompilerParams(dimension_semantics=semantics,
                                vmem_limit_bytes=vmem_limit)


def _first_kernel(x_ref, nw_ref, w_ref, cast_in_ref, h_ref, rs_ref, o_ref, cast_out_ref):
    x = x_ref[...]
    xw = (x * nw_ref[...]).astype(BF16)
    h_ref[...] = xw
    rs = lax.rsqrt(jnp.mean(x * x, axis=-1, keepdims=True) + EPS)
    rs_ref[...] = jnp.broadcast_to(rs, rs_ref.shape)
    o_ref[...] = (jnp.dot(xw, w_ref[...], preferred_element_type=F32) * rs).astype(o_ref.dtype)
    cast_out_ref[...] = cast_in_ref[...].astype(cast_out_ref.dtype)


def _first(x2, nw, w, cast_src, cast_start, cast_len, tm=256):
    t, d = x2.shape
    n = w.shape[1]
    steps = t // tm
    cw = cast_len // steps
    assert cw * steps == cast_len and cast_start % cw == 0
    first = cast_start // cw
    return pl.pallas_call(
        _first_kernel,
        grid=(steps,),
        in_specs=[pl.BlockSpec((tm, d), lambda i: (i, 0)),
                  pl.BlockSpec((1, d), lambda i: (0, 0)),
                  pl.BlockSpec((d, n), lambda i: (0, 0), pipeline_mode=pl.Buffered(1)),
                  pl.BlockSpec((cast_src.shape[0], cw), lambda i: (0, first + i))],
        out_specs=[pl.BlockSpec((tm, d), lambda i: (i, 0)),
                   pl.BlockSpec((tm, GROUP_DIM), lambda i: (i, 0)),
                   pl.BlockSpec((tm, n), lambda i: (i, 0)),
                   pl.BlockSpec((cast_src.shape[0], cw), lambda i: (0, i))],
        out_shape=[jax.ShapeDtypeStruct((t, d), BF16),
                   jax.ShapeDtypeStruct((t, GROUP_DIM), F32),
                   jax.ShapeDtypeStruct((t, n), BF16),
                   jax.ShapeDtypeStruct((cast_src.shape[0], cast_len), BF16)],
        compiler_params=_params("parallel"),
        name="prenorm_proj_inp",
    )(x2, nw, w, cast_src)


def _proj_kernel(*refs, n_w, n_extra, n_out, epilogue):
    a_ref, rs_ref = refs[0], refs[1]
    w_refs = refs[2:2 + n_w]
    extra_refs = refs[2 + n_w:2 + n_w + n_extra]
    rest = refs[2 + n_w + n_extra:]
    a = a_ref[...]
    rs = rs_ref[...]
    rs = jnp.concatenate([rs] * (w_refs[0].shape[1] // rs.shape[1]), axis=1)
    accs = [jnp.dot(a, w[...], preferred_element_type=F32) * rs for w in w_refs]
    outs = epilogue(*accs, *[e[...] for e in extra_refs])
    if len(rest) > n_out:
        cast_in_ref, out_refs, cast_out_ref = rest[0], rest[1:1 + n_out], rest[1 + n_out]
        cast_out_ref[...] = cast_in_ref[...].astype(cast_out_ref.dtype)
    else:
        out_refs = rest
    for o_ref, o in zip(out_refs, outs):
        o_ref[...] = o.astype(o_ref.dtype)


def _proj(h, rs, w, col_offsets, n_cols, epilogue, out_dtypes, extras=(), cast=None, tm=1024, tn=1024,
          name="proj"):
    t, k = h.shape
    ni, nj = t // tm, n_cols // tn
    in_specs = [pl.BlockSpec((tm, k), lambda i, j: (i, 0)),
                pl.BlockSpec((tm, rs.shape[1]), lambda i, j: (i, 0))]
    for off in col_offsets:
        in_specs.append(pl.BlockSpec((k, tn), functools.partial(
            lambda i, j, ob: (0, ob + j), ob=off // tn)))
    for e, off in extras:
        in_specs.append(pl.BlockSpec((e.shape[0], tn), functools.partial(
            lambda i, j, ob: (0, ob + j), ob=off // tn)))
    out_specs = [pl.BlockSpec((tm, tn), lambda i, j: (i, j)) for _ in out_dtypes]
    out_shape = [jax.ShapeDtypeStruct((t, n_cols), dt) for dt in out_dtypes]
    operands = [h, rs, *([w] * len(col_offsets)), *[e for e, _ in extras]]
    if cast is not None:
        src, axis, start, length = cast
        steps = ni * nj
        size = length // steps
        assert size * steps == length and start % size == 0
        first = start // size
        if axis == 0:
            blk = (size, src.shape[1])
            in_specs.append(pl.BlockSpec(blk, lambda i, j: (first + i * nj + j, 0)))
            out_specs.append(pl.BlockSpec(blk, lambda i, j: (i * nj + j, 0)))
            out_shape.append(jax.ShapeDtypeStruct((length, src.shape[1]), BF16))
        else:
            blk = (src.shape[0], size)
            in_specs.append(pl.BlockSpec(blk, lambda i, j: (0, first + i * nj + j)))
            out_specs.append(pl.BlockSpec(blk, lambda i, j: (0, i * nj + j)))
            out_shape.append(jax.ShapeDtypeStruct((src.shape[0], length), BF16))
        operands.append(src)
    return pl.pallas_call(
        functools.partial(_proj_kernel, n_w=len(col_offsets), n_extra=len(extras),
                          n_out=len(out_dtypes), epilogue=epilogue),
        grid=(ni, nj),
        in_specs=in_specs,
        out_specs=out_specs,
        out_shape=out_shape,
        compiler_params=_params("parallel", "parallel",
                                vmem_limit=VMEM_LIMIT_3DOT_BYTES if len(col_offsets) == 3 else VMEM_LIMIT_BYTES),
        name=name,
    )(*operands)


def _sigmoid(a):
    return 0.5 * jnp.tanh(0.5 * a) + 0.5


def _silu(a):
    return a * _sigmoid(a)


def _epi_gelu_silu(u, z):
    return (jax.nn.gelu(u) * _silu(z),)


def _epi_sigmoid(a):
    return (_sigmoid(a),)


def _epi_sigmoid_silu_gain(og, z, gain):
    return (_sigmoid(og) * _silu(z) * gain,)


def _epi_forget(f_logit, lb_logits):
    lb = jax.nn.softmax(lb_logits.astype(F32), axis=0)[0:1, :]
    f = lb + (1.0 - lb) * _sigmoid(f_logit)
    g = jnp.log2(f)
    g_hi = g.astype(BF16)
    return g_hi, g - g_hi.astype(F32), 1.0 - f


def _epi_gelu_silu_forget(v, q, f_logit, lb_logits):
    return (jax.nn.gelu(v), _silu(q), *_epi_forget(f_logit, lb_logits))


def _hgrn_kernel(q_ref, ghi_ref, glo_ref, k_ref, v_ref, gate_ref, o_ref,
                 st_ref, qin_ref, kin_ref, qdec_ref, kdec_ref, dec_ref, p_ref, oi_ref, sbf_ref,
                 *, n_heads):
    ts = q_ref.shape[0]
    nc = ts // CHUNK

    @pl.when(pl.program_id(1) == 0)
    def _():
        st_ref[...] = jnp.zeros_like(st_ref)

    row = lax.broadcasted_iota(jnp.int32, (ts, ts), 0)
    col = lax.broadcasted_iota(jnp.int32, (ts, ts), 1)
    tri = jnp.where(((row // CHUNK) == (col // CHUNK)) & (col <= row), 1.0, 0.0).astype(BF16)
    gw = GROUP_DIM * 4
    for grp in range(q_ref.shape[1] // gw):
        gc = slice(grp * gw, (grp + 1) * gw)
        c_all = (jnp.dot(tri, ghi_ref[:, gc], preferred_element_type=F32)
                 + jnp.dot(tri, glo_ref[:, gc], preferred_element_type=F32))
        for n in range(nc):
            rows = slice(n * CHUNK, (n + 1) * CHUNK)
            c = c_all[n * CHUNK:(n + 1) * CHUNK]
            c_mid = c[CHUNK // 2 - 1:CHUNK // 2, :]
            c_last = c[CHUNK - 1:CHUNK, :]
            q_in = q_ref[rows, gc] * jnp.exp2(c - c_mid).astype(BF16)
            k_in = k_ref[rows, gc] * jnp.exp2(c_mid - c).astype(BF16)
            qin_ref[rows, gc] = q_in
            kin_ref[rows, gc] = k_in
            qdec_ref[rows, gc] = q_in * jnp.exp2(c_mid).astype(BF16)
            kdec_ref[rows, gc] = k_in * jnp.exp2(c_last - c_mid).astype(BF16)
            dec_ref[n:n + 1, gc] = jnp.exp2(c_last)

    ri = lax.broadcasted_iota(jnp.int32, (CHUNK, CHUNK), 0)
    ci = lax.broadcasted_iota(jnp.int32, (CHUNK, CHUNK), 1)
    causal = ci <= ri

    for n in range(nc):
        rows = slice(n * CHUNK, (n + 1) * CHUNK)
        for h in range(n_heads):
            cols = slice(h * GROUP_DIM, (h + 1) * GROUP_DIM)
            scores = lax.dot_general(qin_ref[rows, cols], kin_ref[rows, cols],
                                     (((1,), (1,)), ((), ())), preferred_element_type=F32)
            p_ref[h, rows, :] = jnp.where(causal, scores, 0.0).astype(BF16)

    for n in range(nc):
        rows = slice(n * CHUNK, (n + 1) * CHUNK)
        for h in range(n_heads):
            cols = slice(h * GROUP_DIM, (h + 1) * GROUP_DIM)
            v = v_ref[rows, cols]
            oi_ref[rows, cols] = jnp.dot(p_ref[h, rows, :], v, preferred_element_type=F32)
            upd = lax.dot_general(v, kdec_ref[rows, cols], (((0,), (0,)), ((), ())),
                                  preferred_element_type=F32)
            st = st_ref[h]
            sbf_ref[n, h] = st.astype(BF16).T
            st_ref[h] = st * dec_ref[n:n + 1, cols] + upd

    for n in range(nc):
        rows = slice(n * CHUNK, (n + 1) * CHUNK)
        for h in range(n_heads):
            cols = slice(h * GROUP_DIM, (h + 1) * GROUP_DIM)
            o = oi_ref[rows, cols] + jnp.dot(qdec_ref[rows, cols], sbf_ref[n, h],
                                             preferred_element_type=F32)
            ms = jnp.mean(o * o, axis=-1, keepdims=True)
            on = o * lax.rsqrt(ms + EPS)
            o_ref[rows, cols] = (on * gate_ref[rows, cols].astype(F32)).astype(o_ref.dtype)


def _hgrn(qf, g_hi, g_lo, k, v, gate, batch, seq, ts=256):
    t, width = qf.shape
    n_heads = width // GROUP_DIM
    steps = seq // ts
    nc = ts // CHUNK
    row_spec = pl.BlockSpec((ts, width), lambda b, s: (b * steps + s, 0))
    return pl.pallas_call(
        functools.partial(_hgrn_kernel, n_heads=n_heads),
        grid=(batch, steps),
        in_specs=[row_spec] * 6,
        out_specs=row_spec,
        out_shape=jax.ShapeDtypeStruct((t, width), BF16),
        scratch_shapes=[pltpu.VMEM((n_heads, GROUP_DIM, GROUP_DIM), F32),
                        pltpu.VMEM((ts, width), BF16), pltpu.VMEM((ts, width), BF16),
                        pltpu.VMEM((ts, width), BF16), pltpu.VMEM((ts, width), BF16),
                        pltpu.VMEM((8, width), F32),
                        pltpu.VMEM((n_heads, ts, CHUNK), BF16),
                        pltpu.VMEM((ts, width), F32),
                        pltpu.VMEM((nc, n_heads, GROUP_DIM, GROUP_DIM), BF16)],
        compiler_params=_params("arbitrary", "arbitrary"),
        name="hgrn2",
    )(qf, g_hi, g_lo, k, v, gate)


def _merge_gmlp_kernel(v_ref, gatea_ref, lnw_ref, lnb_ref, ws_ref, bs_ref, hb_ref, wa_ref, wb_ref,
                       ga_ref, gb_ref, o_ref, wm_ref, *, n_groups):
    @pl.when(pl.program_id(0) == 0)
    def _():
        row = lax.broadcasted_iota(jnp.int32, (GMLP_BLOCK, GMLP_BLOCK), 0)
        col = lax.broadcasted_iota(jnp.int32, (GMLP_BLOCK, GMLP_BLOCK), 1)
        mask = (col // CHUNK) <= (row // CHUNK)
        for g in range(n_groups):
            wm_ref[g] = jnp.where(mask, ws_ref[g], 0.0).astype(BF16)

    b = jnp.dot(hb_ref[...], wb_ref[...], preferred_element_type=F32)

    v = v_ref[...].astype(F32)
    mu = jnp.mean(v, axis=-1, keepdims=True)
    xc = v - mu
    var = jnp.mean(xc * xc, axis=-1, keepdims=True)
    y = ((xc * lax.rsqrt(var + EPS)) * lnw_ref[...] + lnb_ref[...]).astype(BF16)
    block_rows = []
    for bi in range(v_ref.shape[0] // GMLP_BLOCK):
        rows = slice(bi * GMLP_BLOCK, (bi + 1) * GMLP_BLOCK)
        group_cols = []
        for g in range(n_groups):
            cols = slice(g * GROUP_DIM, (g + 1) * GROUP_DIM)
            mixed = jnp.dot(wm_ref[g], y[rows, cols], preferred_element_type=F32) + bs_ref[:, g:g + 1]
            group_cols.append((gatea_ref[rows, cols].astype(F32) * mixed).astype(BF16))
        block_rows.append(jnp.concatenate(group_cols, axis=1))
    h_a = jnp.concatenate(block_rows, axis=0)

    a = jnp.dot(h_a, wa_ref[...], preferred_element_type=F32)
    o_ref[...] = (ga_ref[...].astype(F32) * a + gb_ref[...].astype(F32) * b).astype(o_ref.dtype)


def _merge_gmlp(gv, gate_a, ln_w, ln_b, w_s, b_s_t, h_b, w_ab, gates, tm=256):
    t, k = h_b.shape
    d = w_ab.shape[1]
    n_groups = w_s.shape[0]
    return pl.pallas_call(
        functools.partial(_merge_gmlp_kernel, n_groups=n_groups),
        grid=(t // tm,),
        in_specs=[pl.BlockSpec((tm, k), lambda i: (i, 0)),
                  pl.BlockSpec((tm, k), lambda i: (i, 0)),
                  pl.BlockSpec((1, k), lambda i: (0, 0)),
                  pl.BlockSpec((1, k), lambda i: (0, 0)),
                  pl.BlockSpec(w_s.shape, lambda i: (0, 0, 0), pipeline_mode=pl.Buffered(1)),
                  pl.BlockSpec(b_s_t.shape, lambda i: (0, 0)),
                  pl.BlockSpec((tm, k), lambda i: (i, 0)),
                  pl.BlockSpec((k, d), lambda i: (0, 0), pipeline_mode=pl.Buffered(1)),
                  pl.BlockSpec((k, d), lambda i: (1, 0), pipeline_mode=pl.Buffered(1)),
                  pl.BlockSpec((tm, d), lambda i: (i, 0)),
                  pl.BlockSpec((tm, d), lambda i: (i, 1))],
        out_specs=pl.BlockSpec((tm, d), lambda i: (i, 0)),
        out_shape=jax.ShapeDtypeStruct((t, d), BF16),
        scratch_shapes=[pltpu.VMEM(w_s.shape, BF16)],
        compiler_params=_params("arbitrary"),
        name="merge_gmlp",
    )(gv, gate_a, ln_w, ln_b, w_s, b_s_t, h_b, w_ab, w_ab, gates, gates)


def _out_kernel(m_ref, w_ref, x_ref, fw_ref, o_ref):
    y = jnp.dot(m_ref[...], w_ref[...], preferred_element_type=F32) + x_ref[...]
    ms = jnp.mean(y * y, axis=-1, keepdims=True)
    o_ref[...] = y * lax.rsqrt(ms + EPS) * fw_ref[...]


def _out_proj(merged, w_out, x2, final_w, tm=256):
    t, d = x2.shape
    return pl.pallas_call(
        _out_kernel,
        grid=(t // tm,),
        in_specs=[pl.BlockSpec((tm, d), lambda i: (i, 0)),
                  pl.BlockSpec((d, d), lambda i: (0, 0), pipeline_mode=pl.Buffered(1)),
                  pl.BlockSpec((tm, d), lambda i: (i, 0)),
                  pl.BlockSpec((1, d), lambda i: (0, 0))],
        out_specs=pl.BlockSpec((tm, d), lambda i: (i, 0)),
        out_shape=jax.ShapeDtypeStruct((t, d), F32),
        compiler_params=_params("parallel"),
        name="outproj",
    )(merged, w_out, x2, final_w)


def kernel(x, norm_w, w_in, gmlp_ln_w, gmlp_ln_b, gmlp_w_s, gmlp_b_s, hgrn_lb_logits,
           hgrn_norm_w, w_branch, w_out, final_norm_w):
    bsz, seq, d = x.shape
    assert norm_w.shape[0] == 1, "single-layer block"
    width = gmlp_ln_w.shape[1]
    t = bsz * seq
    x2 = x.reshape(t, d)

    off = [i * width for i in range(9)]
    w_in2 = w_in[0]
    w_inp = w_in2[:, off[5]:off[6]].astype(BF16)
    h, rs, inp, w_gates = _first(x2, norm_w, w_inp, w_in2, off[8], 2 * d)
    gates, w_rest = _proj(h, rs, w_gates, [0], 2 * d, _epi_sigmoid, [BF16],
                          cast=(w_in2, 1, 0, off[8]), name="proj_gates")
    gate_a, w_o = _proj(h, rs, w_rest, [off[0], off[2]], width, _epi_gelu_silu, [BF16],
                        cast=(w_out[0], 0, 0, d), tn=512, name="proj_uz")
    gate_b, w_ab = _proj(h, rs, w_rest, [off[6], off[7]], width, _epi_sigmoid_silu_gain, [BF16],
                         extras=[(hgrn_norm_w, 0)],
                         cast=(w_branch[0].reshape(2 * width, d), 0, 0, 2 * width), tn=512, name="proj_ogz")
    gv, qf, g_hi, g_lo, k = _proj(h, rs, w_rest, [off[1], off[3], off[4]], width, _epi_gelu_silu_forget,
                                  [BF16] * 5, extras=[(hgrn_lb_logits, 0)], tn=512, name="proj_vqf")

    h_b = _hgrn(qf, g_hi, g_lo, k, inp, gate_b, bsz, seq)
    merged = _merge_gmlp(gv, gate_a, gmlp_ln_w, gmlp_ln_b, gmlp_w_s[0], gmlp_b_s[0].T, h_b, w_ab, gates)
    out = _out_proj(merged, w_o, x2, final_norm_w[None, :])
    return out.reshape(bsz, seq, d)
```
